```python
import jax, jax.numpy as jnp
from jax import lax
import numpy as np


D_MODEL = 1024
BATCH = 2
SEQ = 8192
DEPTH = 1
DEC_BATCH = 128
DEC_SEQ = 4
PAST_LEN = 16384
PAGE_SIZE = 128

MLA_HEADS = 8
QK_NOPE_DIM = 64
QK_ROPE_DIM = 32
QK_HEAD_DIM = QK_NOPE_DIM + QK_ROPE_DIM
V_HEAD_DIM = 64
Q_LORA = 384
KV_LORA = 256
MLA_WIDTH = MLA_HEADS * V_HEAD_DIM
ROPE_THETA = 10000.0
SM_SCALE = QK_HEAD_DIM ** -0.5
Q_BLOCK = 128

POOL_WINDOWS = (2, 4, 8, 16)
POOL_GROUPS = len(POOL_WINDOWS)
POOL_WIDTH = D_MODEL // 2
POOL_GROUP_DIM = POOL_WIDTH // POOL_GROUPS
POOL_STATE = max(POOL_WINDOWS) - 1

MIX_WIDTH = MLA_WIDTH + POOL_WIDTH
IN_WIDTH = Q_LORA + KV_LORA + QK_ROPE_DIM + POOL_WIDTH

PEER_HEADS = 8
N_KEYS = 128
N_EXPERTS = N_KEYS * N_KEYS
PEER_TOPK = 16
PEER_KEY_DIM = 256
PEER_HALF = PEER_KEY_DIM // 2
PEER_BLOCK = 128

EPS = 1e-6

kernel_name = "hybrid_mla_pool_peer_step"


def rms_norm(x, g):
    xf = x.astype(jnp.float32)
    y = xf * lax.rsqrt(jnp.mean(xf * xf, axis=-1, keepdims=True) + EPS)
    return (y * g.astype(jnp.float32)).astype(x.dtype)


def rope_angles(pos):
    half = QK_ROPE_DIM // 2
    inv_freq = ROPE_THETA ** (-jnp.arange(half, dtype=jnp.float32) / half)
    ang = pos.astype(jnp.float32)[:, None] * inv_freq[None, :]
    return jnp.cos(ang), jnp.sin(ang)


def apply_rope(x, cos, sin):
    xf = x.astype(jnp.float32)
    x1, x2 = jnp.split(xf, 2, axis=-1)
    return jnp.concatenate([x1 * cos - x2 * sin, x2 * cos + x1 * sin], axis=-1).astype(x.dtype)


def mixer_front(x, pos, g_attn, w_in, g_q_lat, w_q_up, g_kv_lat, g_q_nope, g_q_rope):
    b, s, _ = x.shape
    h = rms_norm(x, g_attn)
    z = h @ w_in
    o1, o2, o3 = Q_LORA, Q_LORA + KV_LORA, Q_LORA + KV_LORA + QK_ROPE_DIM
    q_lat, c_raw, k_pe, p_in = z[..., :o1], z[..., o1:o2], z[..., o2:o3], z[..., o3:]
    c = rms_norm(c_raw, g_kv_lat)
    cos, sin = rope_angles(pos)
    k_pe_rot = apply_rope(k_pe, cos, sin)
    q = (rms_norm(q_lat, g_q_lat) @ w_q_up).reshape(b, s, MLA_HEADS, QK_HEAD_DIM)
    q = rms_norm(q, jnp.concatenate([g_q_nope, g_q_rope, g_q_rope]))
    q = jnp.concatenate([q[..., :QK_NOPE_DIM],
                         apply_rope(q[..., QK_NOPE_DIM:], cos[:, None, :], sin[:, None, :])], axis=-1)
    return q, c, k_pe_rot, p_in


def mla_keys(c, k_pe_rot, w_uk, g_k_nope, g_k_rope):
    k_nope = jnp.einsum('blc,chn->blhn', c, w_uk)
    k_pe = jnp.broadcast_to(k_pe_rot[:, :, None, :], k_nope.shape[:3] + (QK_ROPE_DIM,))
    return rms_norm(jnp.concatenate([k_nope, k_pe], axis=-1),
                    jnp.concatenate([g_k_nope, g_k_rope, g_k_rope]))


def attn_probs(q, k, mask):
    s = jnp.einsum('bqhd,blhd->bhql', q, k).astype(jnp.float32) * SM_SCALE
    s = jnp.where(mask, s, -jnp.inf)
    return jax.nn.softmax(s, axis=-1)


def mla_prompt(q, c, k_pe_rot, w_uk, w_uv, g_k_nope, g_k_rope):
    b, s = q.shape[:2]
    k = mla_keys(c, k_pe_rot, w_uk, g_k_nope, g_k_rope)
    v = jnp.einsum('blc,chv->blhv', c, w_uv)
    nq = s // Q_BLOCK
    q_blocks = q.reshape(b, nq, Q_BLOCK, MLA_HEADS, QK_HEAD_DIM).transpose(1, 0, 2, 3, 4)
    kpos = jnp.arange(s)

    def block(args):
        qb, i = args
        qpos = i * Q_BLOCK + jnp.arange(Q_BLOCK)
        p = attn_probs(qb, k, kpos[None, :] <= qpos[:, None])
        return jnp.einsum('bhql,blhv->bqhv', p.astype(v.dtype), v)

    o = lax.map(block, (q_blocks, jnp.arange(nq)))
    return o.transpose(1, 0, 2, 3, 4).reshape(b, s, MLA_WIDTH)


def mla_sample(q, c_new, kpe_new, cache_kv_latent, cache_k_rope, page_table,
               w_uk, w_uv, g_k_nope, g_k_rope):
    b, s = q.shape[:2]

    def one_seq(args):
        pages, qb, cb, rb = args
        c_past = cache_kv_latent[pages].reshape(-1, KV_LORA)
        r_past = cache_k_rope[pages].reshape(-1, QK_ROPE_DIM)
        past = c_past.shape[0]
        c_all = jnp.concatenate([c_past, cb.astype(c_past.dtype)], axis=0)[None]
        r_all = jnp.concatenate([r_past, rb.astype(r_past.dtype)], axis=0)[None]
        k = mla_keys(c_all, r_all, w_uk, g_k_nope, g_k_rope)
        mask = jnp.arange(past + s)[None, :] <= (past + jnp.arange(s))[:, None]
        p = attn_probs(qb[None], k, mask)
        o_lat = jnp.einsum('bhql,blc->bqhc', p.astype(c_all.dtype), c_all)
        return jnp.einsum('bqhc,chv->bqhv', o_lat, w_uv)[0]

    o = lax.map(one_seq, (page_table, q, c_new, kpe_new))
    return o.reshape(b, s, MLA_WIDTH)


def pool_mix(p_in, prev, start_pos, w_pool, pool_scale):
    b, s, _ = p_in.shape
    ext = jnp.concatenate([prev.astype(p_in.dtype), p_in], axis=1)
    csum = jnp.pad(jnp.cumsum(ext.astype(jnp.float32), axis=1), ((0, 0), (1, 0), (0, 0)))
    pos = start_pos + jnp.arange(s)
    end = csum[:, POOL_STATE + 1:]
    pooled = []
    for gi, w in enumerate(POOL_WINDOWS):
        lo, hi = gi * POOL_GROUP_DIM, (gi + 1) * POOL_GROUP_DIM
        start = csum[:, POOL_STATE + 1 - w: POOL_STATE + 1 - w + s, lo:hi]
        cnt = jnp.minimum(w, pos + 1).astype(jnp.float32)[None, :, None]
        pooled.append((end[..., lo:hi] - start) / cnt)
    pooled = jnp.concatenate(pooled, axis=-1) - p_in.astype(jnp.float32)
    y = jnp.einsum('bsgi,gio->bsgo',
                   pooled.reshape(b, s, POOL_GROUPS, POOL_GROUP_DIM).astype(p_in.dtype), w_pool)
    return y.reshape(b, s, POOL_WIDTH) * pool_scale, ext[:, -POOL_STATE:]


def peer_block(hb, w_pq, peer_keys, peer_u, peer_v):
    t = hb.shape[0]
    q = (hb @ w_pq).reshape(t, PEER_HEADS, 2, PEER_HALF)
    s = jnp.einsum('thpk,hpnk->thpn', q, peer_keys).astype(jnp.float32)
    sv, si = lax.top_k(s, PEER_TOPK)
    cand = sv[:, :, 0, :, None] + sv[:, :, 1, None, :]
    cidx = si[:, :, 0, :, None] * N_KEYS + si[:, :, 1, None, :]
    fv, fi = lax.top_k(cand.reshape(t, PEER_HEADS, PEER_TOPK * PEER_TOPK), PEER_TOPK)
    e = jnp.take_along_axis(cidx.reshape(t, PEER_HEADS, PEER_TOPK * PEER_TOPK), fi, axis=-1)
    g = jax.nn.softmax(fv, axis=-1)
    a = jax.nn.gelu(jnp.einsum('thkd,td->thk', peer_u[e], hb).astype(jnp.float32), approximate=False)
    return jnp.einsum('thk,thkd->td', (g * a).astype(hb.dtype), peer_v[e])


def peer(h, w_pq, peer_keys, peer_u, peer_v):
    b, s, d = h.shape
    n = b * s
    nb = -(-n // PEER_BLOCK)
    t = jnp.pad(h.reshape(n, d), ((0, nb * PEER_BLOCK - n), (0, 0)))
    out = lax.map(lambda tb: peer_block(tb, w_pq, peer_keys, peer_u, peer_v),
                  t.reshape(nb, PEER_BLOCK, d))
    return out.reshape(nb * PEER_BLOCK, d)[:n].reshape(b, s, d)


def setup_inputs(seed: int = 0) -> dict:
    key = jax.random.key(seed)
    ks = jax.random.split(key, 32)
    f32 = jnp.float32
    n_pages = PAST_LEN // PAGE_SIZE
    used = DEC_BATCH * n_pages
    n_phys = used + max(used // 4, 1)

    def nrm(k, shape, scale):
        return jax.random.normal(k, shape, f32) * scale

    def gain(k, n):
        return 1.0 + 0.05 * jax.random.normal(k, (n,), f32)

    page_table = jax.random.permutation(ks[5], n_phys)[:used].reshape(DEC_BATCH, n_pages).astype(jnp.int32)
    return {
        'x_prompt': nrm(ks[0], (BATCH, SEQ, D_MODEL), 1.0),
        'x_sample': nrm(ks[1], (DEC_BATCH, DEC_SEQ, D_MODEL), 1.0),
        'cache_kv_latent': nrm(ks[2], (n_phys, PAGE_SIZE, KV_LORA), 1.0),
        'cache_k_rope': nrm(ks[3], (n_phys, PAGE_SIZE, QK_ROPE_DIM), 1.0),
        'state_pool': nrm(ks[4], (DEC_BATCH, POOL_STATE, POOL_WIDTH), 1.0),
        'page_table': page_table,
        'g_attn': gain(ks[6], D_MODEL),
        'w_in': nrm(ks[7], (D_MODEL, IN_WIDTH), D_MODEL ** -0.5),
        'g_q_lat': gain(ks[8], Q_LORA),
        'w_q_up': nrm(ks[9], (Q_LORA, MLA_HEADS * QK_HEAD_DIM), Q_LORA ** -0.5),
        'g_kv_lat': gain(ks[10], KV_LORA),
        'g_q_nope': gain(ks[11], QK_NOPE_DIM),
        'g_q_rope': gain(ks[12], QK_ROPE_DIM // 2),
        'g_k_nope': gain(ks[13], QK_NOPE_DIM),
        'g_k_rope': gain(ks[14], QK_ROPE_DIM // 2),
        'w_uk': nrm(ks[15], (KV_LORA, MLA_HEADS, QK_NOPE_DIM), KV_LORA ** -0.5),
        'w_uv': nrm(ks[16], (KV_LORA, MLA_HEADS, V_HEAD_DIM), KV_LORA ** -0.5),
        'w_pool': nrm(ks[17], (POOL_GROUPS, POOL_GROUP_DIM, POOL_GROUP_DIM), POOL_GROUP_DIM ** -0.5),
        'pool_scale': gain(ks[18], POOL_WIDTH),
        'w_o': nrm(ks[19], (MIX_WIDTH, D_MODEL), MIX_WIDTH ** -0.5),
        'g_ffn': gain(ks[20], D_MODEL),
        'w_pq': nrm(ks[21], (D_MODEL, PEER_HEADS * PEER_KEY_DIM), D_MODEL ** -0.5),
        'peer_keys': nrm(ks[22], (PEER_HEADS, 2, N_KEYS, PEER_HALF), PEER_HALF ** -0.5),
        'peer_u': nrm(ks[23], (N_EXPERTS, D_MODEL), D_MODEL ** -0.5),
        'peer_v': nrm(ks[24], (N_EXPERTS, D_MODEL), 0.5),
    }


def reference(x_prompt, x_sample, cache_kv_latent, cache_k_rope, state_pool, page_table,
              g_attn, w_in, g_q_lat, w_q_up, g_kv_lat, g_q_nope, g_q_rope, g_k_nope, g_k_rope,
              w_uk, w_uv, w_pool, pool_scale, w_o, g_ffn, w_pq, peer_keys, peer_u, peer_v):
    yp, ys = x_prompt, x_sample
    past_len = page_table.shape[1] * cache_kv_latent.shape[1]
    for _ in range(DEPTH):
        pos_p = jnp.arange(yp.shape[1])
        q, kv_prompt, krope_prompt, p_in = mixer_front(yp, pos_p, g_attn, w_in, g_q_lat, w_q_up,
                                                       g_kv_lat, g_q_nope, g_q_rope)
        a = mla_prompt(q, kv_prompt, krope_prompt, w_uk, w_uv, g_k_nope, g_k_rope)
        zero_prev = jnp.zeros((yp.shape[0], POOL_STATE, POOL_WIDTH), p_in.dtype)
        pool_out, pool_prompt = pool_mix(p_in, zero_prev, 0, w_pool, pool_scale)
        yp = yp + jnp.concatenate([a, pool_out], axis=-1) @ w_o
        yp = yp + peer(rms_norm(yp, g_ffn), w_pq, peer_keys, peer_u, peer_v)

        pos_s = past_len + jnp.arange(ys.shape[1])
        q, kv_sample, krope_sample, p_in = mixer_front(ys, pos_s, g_attn, w_in, g_q_lat, w_q_up,
                                                       g_kv_lat, g_q_nope, g_q_rope)
        a = mla_sample(q, kv_sample, krope_sample, cache_kv_latent, cache_k_rope, page_table,
                       w_uk, w_uv, g_k_nope, g_k_rope)
        pool_out, pool_sample = pool_mix(p_in, state_pool, past_len, w_pool, pool_scale)
        ys = ys + jnp.concatenate([a, pool_out], axis=-1) @ w_o
        ys = ys + peer(rms_norm(ys, g_ffn), w_pq, peer_keys, peer_u, peer_v)
    return (yp, ys, kv_prompt, krope_prompt, pool_prompt, kv_sample, krope_sample, pool_sample)
```

```python
import functools

import jax
import jax.numpy as jnp
from jax import lax
from jax.experimental import pallas as pl
from jax.experimental.pallas import tpu as pltpu

F32 = jnp.float32
BF16 = jnp.bfloat16

MLA_HEADS = 8
QK_NOPE_DIM = 64
QK_ROPE_DIM = 32
QK_HEAD_DIM = QK_NOPE_DIM + QK_ROPE_DIM
V_HEAD_DIM = 64
Q_LORA = 384
KV_LORA = 256
ROPE_THETA = 10000.0
SM_SCALE = QK_HEAD_DIM ** -0.5
POOL_WINDOWS = (2, 4, 8, 16)
POOL_STATE = max(POOL_WINDOWS) - 1
PEER_HEADS = 8
N_KEYS = 128
PEER_TOPK = 16
PEER_HALF = 128
EPS = 1e-6

LANE = 128
ROPE_LO = QK_NOPE_DIM
ROPE_MID = ROPE_LO + QK_ROPE_DIM // 2
VMEM_LIMIT = 48 * 1024 * 1024

FRONT_TM = 256
ATTN_T = 1024
DEC_PAGES = 16
MIX_TM = 256
ROUTE_TT = 256
EXP_TT = 256
EXP_EC = 1024


def _params(*sem):
    return pltpu.CompilerParams(dimension_semantics=sem, vmem_limit_bytes=VMEM_LIMIT)


def _rms(x, g):
    return x * lax.rsqrt(jnp.mean(x * x, axis=-1, keepdims=True) + EPS) * g


def _slot_rms(x, g):
    ss = jnp.sum(x * x, axis=-1, keepdims=True) * (1.0 / QK_HEAD_DIM)
    return x * lax.rsqrt(ss + EPS) * g


def _rope(x, cos_t, sin_t, lane):
    swapped = jnp.where(lane < ROPE_MID, pltpu.roll(x, LANE - QK_ROPE_DIM // 2, 1),
                        pltpu.roll(x, QK_ROPE_DIM // 2, 1))
    return x * cos_t + swapped * sin_t


def _dot(a, b):
    return jnp.dot(a, b, preferred_element_type=F32)


def _dot_nt(a, b):
    return lax.dot_general(a, b, (((1,), (1,)), ((), ())), preferred_element_type=F32)


def _front_kernel(emit_kv, x_ref, cos_ref, sin_ref, g_attn_ref, w_in_ref, g_qlat_ref, w_qup_ref, g_kv_ref,
                  gq_ref, gk_ref, w_uk_ref, w_uv_ref, kv_ref, kr_ref, pin_ref, q_ref, *kv_out):
    x = x_ref[...]
    h = _rms(x, g_attn_ref[...])
    z = _dot(h.astype(BF16), w_in_ref[...])
    o1, o2, o3 = Q_LORA, Q_LORA + KV_LORA, Q_LORA + KV_LORA + LANE
    q_lat, c_raw, kpe, p_in = z[:, :o1], z[:, o1:o2], z[:, o2:o3], z[:, o3:]
    c = _rms(c_raw, g_kv_ref[...])
    kv_ref[...] = c
    pin_ref[...] = p_in
    cos_t, sin_t = cos_ref[...], sin_ref[...]
    lane = lax.broadcasted_iota(jnp.int32, kpe.shape, 1)
    kpe_rot = _rope(kpe, cos_t, sin_t, lane)
    kr_ref[...] = kpe_rot[:, ROPE_LO:ROPE_LO + QK_ROPE_DIM]
    qn = _rms(q_lat, g_qlat_ref[...]).astype(BF16)
    qf = _dot(qn, w_qup_ref[...])
    gq = gq_ref[...]
    for hh in range(MLA_HEADS):
        sl = slice(hh * LANE, (hh + 1) * LANE)
        q_ref[:, sl] = _rope(_slot_rms(qf[:, sl], gq), cos_t, sin_t, lane).astype(BF16)
    if emit_kv:
        k_ref, v_ref = kv_out
        cb = c.astype(BF16)
        kn = _dot(cb, w_uk_ref[...])
        v_ref[...] = _dot(cb, w_uv_ref[...]).astype(BF16)
        gk = gk_ref[...]
        for hh in range(MLA_HEADS):
            sl = slice(hh * LANE, (hh + 1) * LANE)
            k_ref[:, sl] = _slot_rms(kn[:, sl] + kpe_rot, gk).astype(BF16)


def _front(x, cos_t, sin_t, wts, emit_kv):
    t, d = x.shape
    tm = FRONT_TM
    n_rope = cos_t.shape[0] // tm
    hp = MLA_HEADS * LANE
    row = lambda i: (i, 0)
    fix = lambda i: (0, 0)
    rope_map = lambda i: (i % n_rope, 0)
    full = lambda a: pl.BlockSpec(a.shape, fix)
    weights = [wts['g_attn'], wts['w_in'], wts['g_q_lat'], wts['w_q_up'], wts['g_kv_lat'], wts['gq'], wts['gk'],
               wts['w_uk'], wts['w_uv']]
    out_shape = [jax.ShapeDtypeStruct((t, KV_LORA), F32), jax.ShapeDtypeStruct((t, QK_ROPE_DIM), F32),
                 jax.ShapeDtypeStruct((t, d // 2), F32), jax.ShapeDtypeStruct((t, hp), BF16)]
    out_specs = [pl.BlockSpec((tm, KV_LORA), row), pl.BlockSpec((tm, QK_ROPE_DIM), row),
                 pl.BlockSpec((tm, d // 2), row), pl.BlockSpec((tm, hp), row)]
    if emit_kv:
        out_shape += [jax.ShapeDtypeStruct((t, hp), BF16)] * 2
        out_specs += [pl.BlockSpec((tm, hp), row)] * 2
    return pl.pallas_call(
        functools.partial(_front_kernel, emit_kv),
        out_shape=out_shape,
        grid=(t // tm,),
        in_specs=[pl.BlockSpec((tm, d), row), pl.BlockSpec((tm, LANE), rope_map), pl.BlockSpec((tm, LANE), rope_map)]
        + [full(w) for w in weights],
        out_specs=out_specs,
        compiler_params=_params("parallel"),
        name="front_kv" if emit_kv else "front",
    )(x, cos_t, sin_t, *weights)


def _flash_kernel(q_ref, k_ref, v_ref, o_ref, m_ref, l_ref, acc_ref):
    qi, ki = pl.program_id(2), pl.program_id(3)

    @pl.when(ki == 0)
    def _():
        m_ref[...] = jnp.full_like(m_ref, -jnp.inf)
        l_ref[...] = jnp.zeros_like(l_ref)
        acc_ref[...] = jnp.zeros_like(acc_ref)

    def step(masked):
        s = _dot_nt(q_ref[...], k_ref[...]) * SM_SCALE
        if masked:
            r = lax.broadcasted_iota(jnp.int32, s.shape, 0)
            c = lax.broadcasted_iota(jnp.int32, s.shape, 1)
            s = jnp.where(c <= r, s, -jnp.inf)
        m_old = m_ref[...]
        m_new = jnp.maximum(m_old, jnp.max(s, axis=-1, keepdims=True))
        alpha = jnp.exp(m_old - m_new)
        p = jnp.exp(s - m_new)
        l_ref[...] = alpha * l_ref[...] + jnp.sum(p, axis=-1, keepdims=True)
        acc_ref[...] = alpha * acc_ref[...] + _dot(p.astype(BF16), v_ref[...])
        m_ref[...] = m_new

    @pl.when(ki < qi)
    def _():
        step(False)

    @pl.when(ki == qi)
    def _():
        step(True)
        o_ref[...] = (acc_ref[...] / l_ref[...]).astype(o_ref.dtype)


def _flash(q, k, v):
    b, s, hp = q.shape
    t = ATTN_T
    n = s // t
    qmap = lambda bi, h, qi, ki: (bi, qi, h)
    kmap = lambda bi, h, qi, ki: (bi, jnp.minimum(ki, qi), h)
    return pl.pallas_call(
        _flash_kernel,
        out_shape=jax.ShapeDtypeStruct((b, s, hp), BF16),
        grid=(b, MLA_HEADS, n, n),
        in_specs=[pl.BlockSpec((None, t, LANE), qmap), pl.BlockSpec((None, t, LANE), kmap),
                  pl.BlockSpec((None, t, LANE), kmap)],
        out_specs=pl.BlockSpec((None, t, LANE), qmap),
        scratch_shapes=[pltpu.VMEM((t, 1), F32), pltpu.VMEM((t, 1), F32), pltpu.VMEM((t, LANE), F32)],
        compiler_params=_params("parallel", "parallel", "parallel", "arbitrary"),
        name="flash_prompt",
    )(q, k, v)


def _dec_prep_kernel(q_ref, gk_ref, w_ukt_ref, qc_ref, qr_ref):
    qg = q_ref[...].astype(F32) * gk_ref[...] * SM_SCALE
    qc_ref[...] = _dot(qg.astype(BF16), w_ukt_ref[...]).astype(BF16)
    qr_ref[...] = qg[:, ROPE_LO:ROPE_LO + QK_ROPE_DIM].astype(BF16)


def _dec_prep(q, gk, w_ukt, n_pos, n_seq):
    return pl.pallas_call(
        _dec_prep_kernel,
        out_shape=[jax.ShapeDtypeStruct((n_pos, MLA_HEADS, n_seq, KV_LORA), BF16),
                   jax.ShapeDtypeStruct((n_pos, MLA_HEADS, n_seq, QK_ROPE_DIM), BF16)],
        grid=(n_pos, MLA_HEADS),
        in_specs=[pl.BlockSpec((n_seq, LANE), lambda i, h: (i, h)), pl.BlockSpec((1, LANE), lambda i, h: (0, 0)),
                  pl.BlockSpec((None, LANE, KV_LORA), lambda i, h: (h, 0, 0))],
        out_specs=[pl.BlockSpec((None, None, n_seq, KV_LORA), lambda i, h: (i, h, 0, 0)),
                   pl.BlockSpec((None, None, n_seq, QK_ROPE_DIM), lambda i, h: (i, h, 0, 0))],
        compiler_params=_params("parallel", "parallel"),
        name="decode_prep",
    )(q, gk, w_ukt)


def _decode_kernel(n_pages, n_new, pt_ref, qc_ref, qr_ref, cn_ref, rn_ref, w_uk_ref, w_uv_ref, ind_ref, *rest):
    kv_refs, kr_refs = rest[:n_pages], rest[n_pages:2 * n_pages]
    o_ref, m_ref, l_ref, acc_ref = rest[2 * n_pages:]
    ci = pl.program_id(1)
    qc, qr, ind = qc_ref[...], qr_ref[...], ind_ref[...]
    ones = jnp.ones((qr.shape[0], QK_ROPE_DIM), BF16)

    @pl.when(ci == 0)
    def _():
        m_ref[...] = jnp.full_like(m_ref, -jnp.inf)
        l_ref[...] = jnp.zeros_like(l_ref)
        acc_ref[...] = jnp.zeros_like(acc_ref)

    def attend(c, kr, mask):
        cb, krb = c.astype(BF16), kr.astype(BF16)
        kn = _dot(cb, w_uk_ref[...])
        ss = _dot_nt(ind, (kn * kn).astype(BF16)) + _dot_nt(ones, (kr * kr).astype(BF16))
        s = (_dot_nt(qc, cb) + _dot_nt(qr, krb)) * lax.rsqrt(ss * (1.0 / QK_HEAD_DIM) + EPS)
        if mask is not None:
            s = jnp.where(mask, s, -jnp.inf)
        m_old = m_ref[...]
        m_new = jnp.maximum(m_old, jnp.max(s, axis=-1, keepdims=True))
        alpha = jnp.exp(m_old - m_new)
        p = jnp.exp(s - m_new)
        l_ref[...] = alpha * l_ref[...] + jnp.sum(p, axis=-1, keepdims=True)
        acc_ref[...] = alpha * acc_ref[...] + _dot(p.astype(BF16), cb)
        m_ref[...] = m_new

    for j in range(n_pages):
        attend(kv_refs[j][...], kr_refs[j][...], None)

    @pl.when(ci == pl.num_programs(1) - 1)
    def _():
        rows = qc.shape[0]
        n_pad = cn_ref.shape[0]
        r = lax.broadcasted_iota(jnp.int32, (rows, n_pad), 0) // MLA_HEADS
        col = lax.broadcasted_iota(jnp.int32, (rows, n_pad), 1)
        attend(cn_ref[...], rn_ref[...], (col <= r) & (col < n_new))
        o_lat = (acc_ref[...] / l_ref[...]).astype(BF16)
        full = _dot(o_lat, w_uv_ref[...])
        rh = lax.broadcasted_iota(jnp.int32, full.shape, 0) % MLA_HEADS
        ch = lax.broadcasted_iota(jnp.int32, full.shape, 1) // V_HEAD_DIM
        own = jnp.where(rh == ch, full, 0.0).reshape(n_new, MLA_HEADS, full.shape[1])
        o_ref[...] = jnp.sum(own, axis=1)


def _decode(page_table, qc, qr, c_new, r_new, w_uk, w_uv, ind, cache_kv, cache_kr, n_new):
    n_seq, n_log = page_table.shape
    p = DEC_PAGES
    page, rows = cache_kv.shape[1], qc.shape[1]
    n_pad = c_new.shape[1]
    seq = lambda b, c, pt: (b, 0, 0)
    fix = lambda b, c, pt: (0, 0)

    def page_map(j):
        return lambda b, c, pt: (pt[b, c * p + j], 0, 0)

    in_specs = [pl.BlockSpec((None, rows, KV_LORA), seq), pl.BlockSpec((None, rows, QK_ROPE_DIM), seq),
                pl.BlockSpec((None, n_pad, KV_LORA), seq), pl.BlockSpec((None, n_pad, QK_ROPE_DIM), seq),
                pl.BlockSpec(w_uk.shape, fix), pl.BlockSpec(w_uv.shape, fix), pl.BlockSpec(ind.shape, fix)]
    in_specs += [pl.BlockSpec((None, page, KV_LORA), page_map(j)) for j in range(p)]
    in_specs += [pl.BlockSpec((None, page, QK_ROPE_DIM), page_map(j)) for j in range(p)]
    return pl.pallas_call(
        functools.partial(_decode_kernel, p, n_new),
        out_shape=jax.ShapeDtypeStruct((n_seq, n_new, MLA_HEADS * V_HEAD_DIM), F32),
        grid_spec=pltpu.PrefetchScalarGridSpec(
            num_scalar_prefetch=1,
            grid=(n_seq, n_log // p),
            in_specs=in_specs,
            out_specs=pl.BlockSpec((None, n_new, MLA_HEADS * V_HEAD_DIM), seq),
            scratch_shapes=[pltpu.VMEM((rows, 1), F32), pltpu.VMEM((rows, 1), F32), pltpu.VMEM((rows, KV_LORA), F32)],
        ),
        compiler_params=_params("parallel", "arbitrary"),
        name="decode_attn",
    )(page_table, qc, qr, c_new, r_new, w_uk, w_uv, ind, *([cache_kv] * p), *([cache_kr] * p))


def _pool_project(pooled, w_pool_ref, scale):
    g = len(POOL_WINDOWS)
    gd = pooled[0].shape[1]
    y = [_dot(pooled[gi].astype(BF16), w_pool_ref[gi]) for gi in range(g)]
    return jnp.concatenate(y, axis=-1) * scale


def _mix_tail(x, a_proj, pool_out, w_op_ref, g_ffn_ref, y_ref, h_ref):
    y = x + (a_proj + _dot(pool_out.astype(BF16), w_op_ref[...]))
    y_ref[...] = y
    h_ref[...] = _rms(y, g_ffn_ref[...]).astype(BF16)


def _mix_prompt_kernel(tiles_per_seq, x_ref, a_ref, pin_ref, halo_ref, w_oa_ref, w_op_ref, w_pool_ref, scale_ref,
                       g_ffn_ref, y_ref, h_ref, ext_ref):
    i = pl.program_id(0)
    tm = pin_ref.shape[0]
    hal = halo_ref.shape[0]
    first = (i % tiles_per_seq) == 0
    ext_ref[:hal, :] = jnp.where(first, 0.0, halo_ref[...])
    ext_ref[hal:, :] = pin_ref[...]
    pos = (i % tiles_per_seq) * tm + lax.broadcasted_iota(jnp.int32, (tm, 1), 0)
    gd = pin_ref.shape[1] // len(POOL_WINDOWS)
    pooled = []
    for gi, w in enumerate(POOL_WINDOWS):
        lanes = slice(gi * gd, (gi + 1) * gd)
        acc = ext_ref[hal:, lanes]
        for k in range(1, w):
            acc = acc + ext_ref[hal - k:hal - k + tm, lanes]
        cnt = jnp.minimum(w, pos + 1).astype(F32)
        pooled.append(acc / cnt - pin_ref[:, lanes])
    pool_out = _pool_project(pooled, w_pool_ref, scale_ref[...])
    _mix_tail(x_ref[...], _dot(a_ref[...], w_oa_ref[...]), pool_out, w_op_ref, g_ffn_ref, y_ref, h_ref)


def _mix_prompt(x, a, pin, seq_len, wts):
    t, d = x.shape
    tm = MIX_TM
    hal = 16
    assert POOL_STATE <= hal and tm % hal == 0 and seq_len % tm == 0
    row = lambda i: (i, 0)
    fix2 = lambda i: (0, 0)
    weights = [wts['w_o_a_pad'], wts['w_o_p'], wts['w_pool'], wts['pool_scale'], wts['g_ffn']]
    w_specs = [pl.BlockSpec(w.shape, (lambda i: (0, 0, 0)) if w.ndim == 3 else fix2) for w in weights]
    return pl.pallas_call(
        functools.partial(_mix_prompt_kernel, seq_len // tm),
        out_shape=[jax.ShapeDtypeStruct((t, d), F32), jax.ShapeDtypeStruct((t, d), BF16)],
        grid=(t // tm,),
        in_specs=[pl.BlockSpec((tm, d), row), pl.BlockSpec((tm, a.shape[1]), row), pl.BlockSpec((tm, pin.shape[1]), row),
                  pl.BlockSpec((hal, pin.shape[1]), lambda i: (jnp.maximum(i * (tm // hal) - 1, 0), 0))] + w_specs,
        out_specs=[pl.BlockSpec((tm, d), row), pl.BlockSpec((tm, d), row)],
        scratch_shapes=[pltpu.VMEM((tm + hal, pin.shape[1]), F32)],
        compiler_params=_params("parallel"),
        name="mix_prompt",
    )(x, a, pin, pin, *weights)


def _mix_sample_kernel(n_seq, n_new, past_len, x_ref, a_ref, pin_ref, st_ref, w_oa_ref, w_op_ref, w_pool_ref, scale_ref,
                       g_ffn_ref, y_ref, h_ref):
    gd = pin_ref.shape[1] // len(POOL_WINDOWS)

    def ext(p, lanes):
        if p < POOL_STATE:
            return st_ref[p * n_seq:(p + 1) * n_seq, lanes]
        return pin_ref[(p - POOL_STATE) * n_seq:(p - POOL_STATE + 1) * n_seq, lanes]

    pooled = []
    for gi, w in enumerate(POOL_WINDOWS):
        lanes = slice(gi * gd, (gi + 1) * gd)
        per_pos = []
        for i in range(n_new):
            acc = ext(POOL_STATE + i, lanes)
            for k in range(1, w):
                acc = acc + ext(POOL_STATE + i - k, lanes)
            cnt = float(min(w, past_len + i + 1))
            per_pos.append(acc / cnt - ext(POOL_STATE + i, lanes))
        pooled.append(jnp.concatenate(per_pos, axis=0))
    pool_out = _pool_project(pooled, w_pool_ref, scale_ref[...])
    _mix_tail(x_ref[...], _dot(a_ref[...].astype(BF16), w_oa_ref[...]), pool_out, w_op_ref, g_ffn_ref, y_ref, h_ref)


def _mix_sample(x, a, pin, state_t, n_seq, n_new, past_len, wts):
    t, d = x.shape
    weights = [wts['w_o_a'], wts['w_o_p'], wts['w_pool'], wts['pool_scale'], wts['g_ffn']]
    args = [x, a, pin, state_t] + weights
    return pl.pallas_call(
        functools.partial(_mix_sample_kernel, n_seq, n_new, past_len),
        out_shape=[jax.ShapeDtypeStruct((t, d), F32), jax.ShapeDtypeStruct((t, d), BF16)],
        grid=(1,),
        in_specs=[pl.BlockSpec(v.shape, (lambda i: (0, 0, 0)) if v.ndim == 3 else (lambda i: (0, 0))) for v in args],
        out_specs=[pl.BlockSpec((t, d), lambda i: (0, 0)), pl.BlockSpec((t, d), lambda i: (0, 0))],
        compiler_params=_params("arbitrary"),
        name="mix_sample",
    )(*args)


def _top_rows(x, k):
    rows = []
    for _ in range(k):
        m = jnp.max(x, axis=0, keepdims=True)
        rows.append(m)
        x = jnp.where(x == m, -jnp.inf, x)
    return rows


def _route_kernel(h_ref, w_pq_ref, keys_ref, s2_ref, th_ref, e2_ref, e1_ref):
    qp = _dot(h_ref[...], w_pq_ref[...]).astype(BF16)
    for hh in range(PEER_HEADS):
        s1 = _dot_nt(keys_ref[2 * hh], qp[:, (2 * hh) * PEER_HALF:(2 * hh + 1) * PEER_HALF])
        s2 = _dot_nt(keys_ref[2 * hh + 1], qp[:, (2 * hh + 1) * PEER_HALF:(2 * hh + 2) * PEER_HALF])
        top1, top2 = _top_rows(s1, PEER_TOPK), _top_rows(s2, PEER_TOPK)
        top2_mat = jnp.concatenate(top2, axis=0)
        cand = jnp.concatenate([r + top2_mat for r in top1], axis=0)
        tau = _top_rows(cand, PEER_TOPK)[-1]
        norm = jnp.sum(jnp.where(cand >= tau, jnp.exp(cand - (top1[0] + top2[0])), 0.0), axis=0, keepdims=True)
        theta = jnp.full_like(s1, jnp.inf)
        for r in top2:
            theta = jnp.where(s1 + r >= tau, r, theta)
        s2_ref[hh] = s2
        th_ref[hh] = theta
        e2_ref[hh] = jnp.exp(s2 - top2[0])
        e1_ref[hh] = jnp.exp(s1 - top1[0]) / norm


def _route(h, w_pq, keys):
    t, d = h.shape
    tt = ROUTE_TT
    out = jax.ShapeDtypeStruct((PEER_HEADS, N_KEYS, t), F32)
    ospec = pl.BlockSpec((PEER_HEADS, N_KEYS, tt), lambda i: (0, 0, i))
    return pl.pallas_call(
        _route_kernel,
        out_shape=[out] * 4,
        grid=(t // tt,),
        in_specs=[pl.BlockSpec((tt, d), lambda i: (i, 0)), pl.BlockSpec(w_pq.shape, lambda i: (0, 0)),
                  pl.BlockSpec(keys.shape, lambda i: (0, 0, 0))],
        out_specs=[ospec] * 4,
        compiler_params=_params("parallel"),
        name="peer_route",
    )(h, w_pq, keys)


def _gelu(x):
    return 0.5 * x * (1.0 + lax.erf(x * (2.0 ** -0.5)))


def _experts_kernel(h_ref, y_ref, u_ref, v_ref, s2_ref, th_ref, e2_ref, e1_ref, o_ref, act_ref, p_ref):
    ci = pl.program_id(1)
    groups = u_ref.shape[0] // N_KEYS

    @pl.when(ci == 0)
    def _():
        o_ref[...] = y_ref[...]

    act_ref[...] = _dot_nt(u_ref[...], h_ref[...])

    def group(gl, carry):
        i = ci * groups + gl
        w = jnp.zeros((N_KEYS, act_ref.shape[1]), F32)
        for hh in range(PEER_HEADS):
            theta = th_ref[hh, pl.ds(i, 1), :]
            e1 = e1_ref[hh, pl.ds(i, 1), :]
            w = w + jnp.where(s2_ref[hh] >= theta, e2_ref[hh], 0.0) * e1
        rows = pl.ds(pl.multiple_of(gl * N_KEYS, N_KEYS), N_KEYS)
        p_ref[rows, :] = (_gelu(act_ref[rows, :]) * w).astype(BF16)
        return carry

    lax.fori_loop(0, groups, group, 0)
    o_ref[...] += lax.dot_general(p_ref[...], v_ref[...], (((0,), (0,)), ((), ())), preferred_element_type=F32)


def _experts(h, y, u, v, s2, th, e2, e1):
    t, d = h.shape
    tt, ec = EXP_TT, EXP_EC
    n_exp = u.shape[0]
    tok = lambda i, c: (i, 0)
    exp = lambda i, c: (c, 0)
    rt = pl.BlockSpec((PEER_HEADS, N_KEYS, tt), lambda i, c: (0, 0, i))
    return pl.pallas_call(
        _experts_kernel,
        out_shape=jax.ShapeDtypeStruct((t, d), F32),
        grid=(t // tt, n_exp // ec),
        in_specs=[pl.BlockSpec((tt, d), tok), pl.BlockSpec((tt, d), tok), pl.BlockSpec((ec, d), exp),
                  pl.BlockSpec((ec, d), exp), rt, rt, rt, rt],
        out_specs=pl.BlockSpec((tt, d), tok),
        scratch_shapes=[pltpu.VMEM((ec, tt), F32), pltpu.VMEM((ec, tt), BF16)],
        compiler_params=_params("parallel", "arbitrary"),
        name="peer_experts",
    )(h, y, u, v, s2, th, e2, e1)


def _peer(h, y, wts):
    s2, th, e2, e1 = _route(h, wts['w_pq'], wts['peer_keys'])
    return _experts(h, y, wts['peer_u'], wts['peer_v'], s2, th, e2, e1)


def _pad_heads(w, width):
    pad = [(0, 0)] * (w.ndim - 1) + [(0, width - w.shape[-1])]
    w = jnp.pad(w, pad)
    return w.reshape(w.shape[:-2] + (w.shape[-2] * width,))


def _rope_tables(pos):
    half = QK_ROPE_DIM // 2
    inv_freq = ROPE_THETA ** (-jnp.arange(half, dtype=F32) / half)
    ang = pos.astype(F32)[:, None] * inv_freq[None, :]
    cos, sin = jnp.cos(ang), jnp.sin(ang)
    n = pos.shape[0]
    tail = jnp.zeros((n, LANE - ROPE_LO - QK_ROPE_DIM), F32)
    cos_t = jnp.concatenate([jnp.ones((n, ROPE_LO), F32), cos, cos, tail], axis=-1)
    sin_t = jnp.concatenate([jnp.zeros((n, ROPE_LO), F32), -sin, sin, tail], axis=-1)
    return cos_t, sin_t


def _prepare_weights(g_attn, w_in, g_q_lat, w_q_up, g_kv_lat, g_q_nope, g_q_rope, g_k_nope, g_k_rope, w_uk, w_uv,
                     w_pool, pool_scale, w_o, g_ffn, w_pq, peer_keys, peer_u, peer_v):
    d = w_in.shape[0]
    o2 = Q_LORA + KV_LORA
    o3 = o2 + QK_ROPE_DIM
    zeros = lambda n: jnp.zeros((d, n), w_in.dtype)
    w_in_pad = jnp.concatenate([w_in[:, :o2], zeros(ROPE_LO), w_in[:, o2:o3], zeros(LANE - ROPE_LO - QK_ROPE_DIM),
                                w_in[:, o3:]], axis=1)
    gain = lambda gn, gr: jnp.concatenate([gn, gr, gr, jnp.zeros((LANE - QK_HEAD_DIM,), F32)])[None, :]
    mla_w = MLA_HEADS * V_HEAD_DIM
    w_ukt = jnp.pad(w_uk.transpose(1, 2, 0), ((0, 0), (0, LANE - QK_NOPE_DIM), (0, 0)))
    return {
        'g_attn': g_attn[None, :], 'w_in': w_in_pad.astype(BF16), 'g_q_lat': g_q_lat[None, :],
        'w_q_up': _pad_heads(w_q_up.reshape(Q_LORA, MLA_HEADS, QK_HEAD_DIM), LANE).astype(BF16),
        'g_kv_lat': g_kv_lat[None, :], 'gq': gain(g_q_nope, g_q_rope), 'gk': gain(g_k_nope, g_k_rope),
        'w_uk': _pad_heads(w_uk, LANE).astype(BF16), 'w_uv': _pad_heads(w_uv, LANE).astype(BF16),
        'w_uk_flat': w_uk.reshape(KV_LORA, -1).astype(BF16), 'w_uv_flat': w_uv.reshape(KV_LORA, -1).astype(BF16),
        'w_ukt': w_ukt.astype(BF16),
        'w_pool': w_pool.astype(BF16), 'pool_scale': pool_scale[None, :],
        'w_o_a': w_o[:mla_w].astype(BF16),
        'w_o_a_pad': jnp.pad(w_o[:mla_w].reshape(MLA_HEADS, V_HEAD_DIM, -1), ((0, 0), (0, LANE - V_HEAD_DIM), (0, 0)))
        .reshape(MLA_HEADS * LANE, -1).astype(BF16),
        'w_o_p': w_o[mla_w:].astype(BF16), 'g_ffn': g_ffn[None, :],
        'w_pq': w_pq.astype(BF16),
        'peer_keys': peer_keys.reshape(PEER_HEADS * 2, N_KEYS, PEER_HALF).astype(BF16),
        'peer_u': peer_u.astype(BF16), 'peer_v': peer_v.astype(BF16),
    }


def kernel(x_prompt, x_sample, cache_kv_latent, cache_k_rope, state_pool, page_table, g_attn, w_in, g_q_lat, w_q_up,
           g_kv_lat, g_q_nope, g_q_rope, g_k_nope, g_k_rope, w_uk, w_uv, w_pool, pool_scale, w_o, g_ffn, w_pq,
           peer_keys, peer_u, peer_v):
    wts = _prepare_weights(g_attn, w_in, g_q_lat, w_q_up, g_kv_lat, g_q_nope, g_q_rope, g_k_nope, g_k_rope, w_uk,
                           w_uv, w_pool, pool_scale, w_o, g_ffn, w_pq, peer_keys, peer_u, peer_v)
    b, s, d = x_prompt.shape
    n_seq, n_new, _ = x_sample.shape
    past_len = page_table.shape[1] * cache_kv_latent.shape[1]

    xp = x_prompt.reshape(b * s, d)
    cos_p, sin_p = _rope_tables(jnp.arange(s))
    kv_p, kr_p, pin_p, q_p, k_p, v_p = _front(xp, cos_p, sin_p, wts, True)
    hp = MLA_HEADS * LANE
    a_p = _flash(q_p.reshape(b, s, hp), k_p.reshape(b, s, hp), v_p.reshape(b, s, hp)).reshape(b * s, hp)
    y1_p, h_p = _mix_prompt(xp, a_p, pin_p, s, wts)
    y_p = _peer(h_p, y1_p, wts)
    pin_p3 = pin_p.reshape(b, s, -1)

    t_s = n_seq * n_new
    to_pm = lambda v: v.transpose(1, 0, 2).reshape(t_s, v.shape[-1])
    from_pm = lambda v: v.reshape(n_new, n_seq, v.shape[-1]).transpose(1, 0, 2)
    xs = to_pm(x_sample)
    cos_s, sin_s = _rope_tables(jnp.repeat(past_len + jnp.arange(n_new), n_seq))
    kv_s, kr_s, pin_s, q_s = _front(xs, cos_s, sin_s, wts, False)
    qc, qr = _dec_prep(q_s, wts['gk'], wts['w_ukt'], n_new, n_seq)
    rows = n_new * MLA_HEADS
    qc = qc.transpose(2, 0, 1, 3).reshape(n_seq, rows, KV_LORA)
    qr = qr.transpose(2, 0, 1, 3).reshape(n_seq, rows, QK_ROPE_DIM)
    kv_s3, kr_s3 = from_pm(kv_s), from_pm(kr_s)
    n_pad = 8
    padn = lambda v: jnp.pad(v, ((0, 0), (0, n_pad - n_new), (0, 0)))
    ind = (jnp.arange(MLA_HEADS * QK_NOPE_DIM)[None, :] // QK_NOPE_DIM) == (jnp.arange(rows)[:, None] % MLA_HEADS)
    a_s = _decode(page_table, qc, qr, padn(kv_s3), padn(kr_s3), wts['w_uk_flat'], wts['w_uv_flat'], ind.astype(BF16),
                  cache_kv_latent, cache_k_rope, n_new)
    state_t = state_pool.transpose(1, 0, 2).reshape(POOL_STATE * n_seq, -1)
    y1_s, h_s = _mix_sample(xs, to_pm(a_s), pin_s, state_t, n_seq, n_new, past_len, wts)
    y_s = _peer(h_s, y1_s, wts)
    pin_s3 = from_pm(pin_s)
    pool_s = jnp.concatenate([state_pool, pin_s3], axis=1)[:, -POOL_STATE:]

    return (y_p.reshape(b, s, d), from_pm(y_s), kv_p.reshape(b, s, -1), kr_p.reshape(b, s, -1),
            pin_p3[:, -POOL_STATE:], kv_s3, kr_s3, pool_s)
```
